```python
import math
import jax, jax.numpy as jnp
from jax import lax
import numpy as np

D_MODEL = 1024
BATCH = 16
SEQ = 2048
DEPTH = 2
DEC_BATCH = 16
DEC_SEQ = 4096
PAST_LEN = 128

HEAD_DIM = 64
ROT_DIM = HEAD_DIM // 4
ROPE_THETA = 500000.0
QBLK = 128
A_HEADS = D_MODEL // 256
A_QK = A_HEADS * 2 * HEAD_DIM
A_V = A_HEADS * 2 * HEAD_DIM
A_OUT = A_V
DIL_PAIRS = ((128, 1), (512, 4), (2048, 16))
N_DIL = len(DIL_PAIRS)
B_HEADS = D_MODEL // 128
B_QKV = N_DIL * B_HEADS * HEAD_DIM
B_OUT = B_HEADS * HEAD_DIM
BAND_BLK = 64
SSM_HEADS = D_MODEL // 64
SSM_HEAD_DIM = 64
SSM_GROUPS = 2
SSM_STATE = 128
SSM_INNER = SSM_HEADS * SSM_HEAD_DIM
SSM_CONV = 5
SSM_CHUNK = 128
C_XBC = SSM_INNER + 2 * SSM_GROUPS * SSM_STATE
C_DT = 2 * SSM_HEADS
DT_MIN = 0.001
DT_MAX = 0.1
N_BRANCH = 3
GATE_W = N_BRANCH * D_MODEL
SPLITS = (A_QK, A_QK, A_V, B_QKV, B_QKV, B_QKV, SSM_INNER, C_XBC, C_DT, GATE_W)
W_IN = sum(SPLITS)
FFN_HIDDEN = -(-8 * D_MODEL // (3 * 256)) * 256
PLE_DIM = 256
EPS = 1e-6
NEG = -1e30

kernel_name = 'hybrid_bidir_encoder_diffattn_dilated_ssd'


def rmsnorm(x, g):
    xf = x.astype(jnp.float32)
    y = xf * lax.rsqrt(jnp.mean(xf * xf, axis=-1, keepdims=True) + EPS)
    return (y * g.astype(jnp.float32)).astype(x.dtype)


def split_points():
    pts, acc = [], 0
    for s in SPLITS[:-1]:
        acc += s
        pts.append(acc)
    return pts


def rope_tables(seq):
    inv = ROPE_THETA ** (-jnp.arange(0, ROT_DIM, 2, dtype=jnp.float32) / ROT_DIM)
    ang = jnp.arange(seq, dtype=jnp.float32)[:, None] * inv[None, :]
    return jnp.cos(ang), jnp.sin(ang)


def apply_rope(x, cos, sin):
    half = ROT_DIM // 2
    xr = x[..., :ROT_DIM].astype(jnp.float32)
    x1, x2 = xr[..., :half], xr[..., half:]
    c = cos[:, None, :]
    s = sin[:, None, :]
    rot = jnp.concatenate([x1 * c - x2 * s, x2 * c + x1 * s], axis=-1).astype(x.dtype)
    return jnp.concatenate([rot, x[..., ROT_DIM:]], axis=-1)


def diff_attention(q, k, v, lam):
    bsz, seq, nh, _, dh = q.shape
    nb = seq // QBLK
    qb = jnp.moveaxis(q.reshape(bsz, nb, QBLK, nh, 2, dh), 1, 0)
    scale = dh ** -0.5

    def one_block(qi):
        s = jnp.einsum('bqhcd,bkhcd->bhcqk', qi, k).astype(jnp.float32) * scale
        pr = jax.nn.softmax(s, axis=-1)
        a = pr[:, :, 0] - lam * pr[:, :, 1]
        return jnp.einsum('bhqk,bkhe->bqhe', a.astype(v.dtype), v)

    o = lax.map(one_block, qb)
    return jnp.moveaxis(o, 0, 1).reshape(bsz, seq, nh, 2 * dh)


def banded_attention(q, k, v, radius):
    n, L, nh, dh = q.shape
    blk = BAND_BLK
    nb = -(-L // blk)
    lp = nb * blk
    ns = -(-radius // blk)
    qp = jnp.pad(q, ((0, 0), (0, lp - L), (0, 0), (0, 0))).reshape(n, nb, blk, nh, dh)
    padk = ((0, 0), (ns * blk, lp - L + ns * blk), (0, 0), (0, 0))
    kp = jnp.pad(k, padk).reshape(n, nb + 2 * ns, blk, nh, dh)
    vp = jnp.pad(v, padk).reshape(n, nb + 2 * ns, blk, nh, dh)
    kband = jnp.concatenate([kp[:, j:j + nb] for j in range(2 * ns + 1)], axis=2)
    vband = jnp.concatenate([vp[:, j:j + nb] for j in range(2 * ns + 1)], axis=2)
    kw = (2 * ns + 1) * blk
    s = jnp.einsum('nbqhd,nbkhd->nbhqk', qp, kband).astype(jnp.float32) * (dh ** -0.5)
    qpos = jnp.arange(nb)[:, None] * blk + jnp.arange(blk)[None, :]
    kpos = jnp.arange(nb)[:, None] * blk - ns * blk + jnp.arange(kw)[None, :]
    rel = kpos[:, None, :] - qpos[:, :, None]
    valid = (jnp.abs(rel) <= radius) & (kpos[:, None, :] >= 0) & (kpos[:, None, :] < L)
    s = jnp.where(valid[None, :, None], s, NEG)
    m = jnp.max(s, axis=-1, keepdims=True)
    e = jnp.exp(s - m)
    den = jnp.sum(e, axis=-1, keepdims=True)
    lse = (m + jnp.log(den))[..., 0]
    o = jnp.einsum('nbhqk,nbkhd->nbqhd', (e / den).astype(v.dtype), vband)
    o = o.reshape(n, lp, nh, dh)[:, :L]
    lse = lse.transpose(0, 1, 3, 2).reshape(n, lp, nh)[:, :L]
    return o, lse


def dilated_group(q, k, v, dil, radius):
    bsz, seq, nh, dh = q.shape
    L = seq // dil

    def sub(t):
        return t.reshape(bsz, L, dil, nh, dh).transpose(0, 2, 1, 3, 4).reshape(bsz * dil, L, nh, dh)

    o, lse = banded_attention(sub(q), sub(k), sub(v), radius)
    o = o.reshape(bsz, dil, L, nh, dh).transpose(0, 2, 1, 3, 4).reshape(bsz, seq, nh, dh)
    lse = lse.reshape(bsz, dil, L, nh).transpose(0, 2, 1, 3).reshape(bsz, seq, nh)
    return o, lse


def segsum(a):
    t = a.shape[-1]
    cs = jnp.cumsum(a, axis=-1)
    seg = cs[..., :, None] - cs[..., None, :]
    mask = jnp.tril(jnp.ones((t, t), dtype=bool))
    return jnp.where(mask, seg, -jnp.inf)


def ssd_chunked(xdt, adt, bm, cm):
    b, seq, g, e, p = xdt.shape
    n = bm.shape[-1]
    c = seq // SSM_CHUNK
    X = xdt.reshape(b, c, SSM_CHUNK, g, e, p)
    Bc = bm.reshape(b, c, SSM_CHUNK, g, n)
    Cc = cm.reshape(b, c, SSM_CHUNK, g, n)
    A = adt.reshape(b, c, SSM_CHUNK, g, e).transpose(0, 3, 4, 1, 2)
    acs = jnp.cumsum(A, axis=-1)
    lmat = jnp.exp(segsum(A))
    cb = jnp.einsum('bclgn,bcsgn->bgcls', Cc, Bc)
    y_diag = jnp.einsum('bgecls,bcsgep->bclgep', cb[:, :, None] * lmat, X)
    decay_states = jnp.exp(acs[..., -1:] - acs)
    states = jnp.einsum('bclgn,bgecl,bclgep->bcgepn', Bc, decay_states, X)
    chunk_decay = jnp.exp(acs[..., -1])

    def step(h, inp):
        st, dec = inp
        return h * dec[..., None, None] + st, h

    h0 = jnp.zeros((b, g, e, p, n), jnp.float32)
    _, h_in = lax.scan(step, h0, (jnp.moveaxis(states, 1, 0), jnp.moveaxis(chunk_decay, 3, 0)))
    h_in = jnp.moveaxis(h_in, 0, 1)
    y_off = jnp.einsum('bclgn,bcgepn,bgecl->bclgep', Cc, h_in, jnp.exp(acs))
    return (y_diag + y_off).reshape(b, seq, g, e, p)


def depthwise_conv(x, w, bias):
    ch = x.shape[-1]
    pad = (SSM_CONV - 1) // 2
    y = lax.conv_general_dilated(x, w[:, None, :].astype(x.dtype), (1,), [(pad, pad)],
                                 dimension_numbers=('NWC', 'WIO', 'NWC'), feature_group_count=ch)
    return y + bias.astype(x.dtype)


def mamba_mixer(z, xbc, dt_raw, conv_w, conv_b, dt_bias, a_log, ssm_d, ssm_norm_g):
    bsz, seq, _ = xbc.shape
    hpg = SSM_HEADS // SSM_GROUPS
    f32 = jnp.float32
    xbc = jax.nn.silu(depthwise_conv(xbc, conv_w, conv_b))
    xs, bm, cm = jnp.split(xbc, [SSM_INNER, SSM_INNER + SSM_GROUPS * SSM_STATE], axis=-1)
    xs = xs.astype(f32).reshape(bsz, seq, SSM_GROUPS, hpg, SSM_HEAD_DIM)
    bm = bm.astype(f32).reshape(bsz, seq, SSM_GROUPS, SSM_STATE)
    cm = cm.astype(f32).reshape(bsz, seq, SSM_GROUPS, SSM_STATE)
    dt = jax.nn.softplus(dt_raw.astype(f32).reshape(bsz, seq, 2, SSM_HEADS) + dt_bias.astype(f32))
    a = -jnp.exp(a_log.astype(f32))
    dt_f = dt[:, :, 0].reshape(bsz, seq, SSM_GROUPS, hpg)
    dt_b = dt[:, :, 1].reshape(bsz, seq, SSM_GROUPS, hpg)
    a_f = a[0].reshape(SSM_GROUPS, hpg)
    a_b = a[1].reshape(SSM_GROUPS, hpg)
    y_f = ssd_chunked(xs * dt_f[..., None], dt_f * a_f, bm, cm)
    flip = lambda t: jnp.flip(t, axis=1)
    y_b = flip(ssd_chunked(flip(xs * dt_b[..., None]), flip(dt_b * a_b), flip(bm), flip(cm)))
    y = y_f + y_b + ssm_d.astype(f32).reshape(SSM_GROUPS, hpg)[:, :, None] * xs
    y = y.reshape(bsz, seq, SSM_INNER) * jax.nn.silu(z.astype(f32))
    gsz = SSM_INNER // SSM_GROUPS
    y = rmsnorm(y.reshape(bsz, seq, SSM_GROUPS, gsz), ssm_norm_g.reshape(SSM_GROUPS, gsz))
    return y.reshape(bsz, seq, SSM_INNER).astype(z.dtype)


def encoder_layer(x, p_i, li, norm_mix_g, w_in, diff_lambda, diff_subln_g, conv_w, conv_b, dt_bias,
                  a_log, ssm_d, ssm_norm_g, w_br_a, w_br_b, w_br_c, w_out, norm_ffn_g, w_ffn_in,
                  w_ffn_out, norm_ple_g, w_ple_gate, w_ple_proj):
    bsz, seq, _ = x.shape
    f32 = jnp.float32
    cos, sin = rope_tables(seq)
    h = rmsnorm(x, norm_mix_g)
    u = h @ w_in
    a_q, a_k, a_v, b_q, b_k, b_v, c_z, c_xbc, c_dt, g_raw = jnp.split(u, split_points(), axis=-1)

    aq = apply_rope(a_q.reshape(bsz, seq, 2 * A_HEADS, HEAD_DIM), cos, sin).reshape(bsz, seq, A_HEADS, 2, HEAD_DIM)
    ak = apply_rope(a_k.reshape(bsz, seq, 2 * A_HEADS, HEAD_DIM), cos, sin).reshape(bsz, seq, A_HEADS, 2, HEAD_DIM)
    av = a_v.reshape(bsz, seq, A_HEADS, 2 * HEAD_DIM)
    lam_init = 0.8 - 0.6 * math.exp(-0.3 * li)
    lp = diff_lambda.astype(f32)
    lam = jnp.exp(jnp.sum(lp[0] * lp[1])) - jnp.exp(jnp.sum(lp[2] * lp[3])) + lam_init
    oa = diff_attention(aq, ak, av, lam)
    oa = (rmsnorm(oa, diff_subln_g) * (1.0 - lam_init)).reshape(bsz, seq, A_OUT)

    bq = apply_rope(b_q.reshape(bsz, seq, N_DIL * B_HEADS, HEAD_DIM), cos, sin).reshape(bsz, seq, N_DIL, B_HEADS, HEAD_DIM)
    bk = apply_rope(b_k.reshape(bsz, seq, N_DIL * B_HEADS, HEAD_DIM), cos, sin).reshape(bsz, seq, N_DIL, B_HEADS, HEAD_DIM)
    bv = b_v.reshape(bsz, seq, N_DIL, B_HEADS, HEAD_DIM)
    outs, lses = [], []
    for gi, (win, dil) in enumerate(DIL_PAIRS):
        o_g, l_g = dilated_group(bq[:, :, gi], bk[:, :, gi], bv[:, :, gi], dil, win // (2 * dil))
        outs.append(o_g)
        lses.append(l_g)
    wts = jax.nn.softmax(jnp.stack(lses), axis=0)
    ob = jnp.sum(wts[..., None] * jnp.stack(outs).astype(f32), axis=0).astype(x.dtype).reshape(bsz, seq, B_OUT)

    oc = mamba_mixer(c_z, c_xbc, c_dt, conv_w, conv_b, dt_bias, a_log, ssm_d, ssm_norm_g)

    gates = jax.nn.sigmoid(g_raw.astype(f32)).astype(x.dtype).reshape(bsz, seq, N_BRANCH, D_MODEL)
    m = gates[:, :, 0] * (oa @ w_br_a) + gates[:, :, 1] * (ob @ w_br_b) + gates[:, :, 2] * (oc @ w_br_c)
    x = x + m @ w_out

    gt, up = jnp.split(rmsnorm(x, norm_ffn_g) @ w_ffn_in, 2, axis=-1)
    x = x + (jax.nn.silu(gt) * up) @ w_ffn_out

    pg = jax.nn.sigmoid(rmsnorm(x, norm_ple_g) @ w_ple_gate)
    x = x + pg * (p_i @ w_ple_proj)
    return x


def trunk(x, p, norm_mix_g, w_in, diff_lambda, diff_subln_g, conv_w, conv_b, dt_bias, a_log, ssm_d,
          ssm_norm_g, w_br_a, w_br_b, w_br_c, w_out, norm_ffn_g, w_ffn_in, w_ffn_out, norm_ple_g,
          w_ple_gate, w_ple_proj, final_norm_g):
    for li in range(DEPTH):
        x = encoder_layer(x, p[li], li, norm_mix_g[li], w_in[li], diff_lambda[li], diff_subln_g[li],
                          conv_w[li], conv_b[li], dt_bias[li], a_log[li], ssm_d[li], ssm_norm_g[li],
                          w_br_a[li], w_br_b[li], w_br_c[li], w_out[li], norm_ffn_g[li], w_ffn_in[li],
                          w_ffn_out[li], norm_ple_g[li], w_ple_gate[li], w_ple_proj[li])
    return rmsnorm(x, final_norm_g)


def setup_inputs(seed: int = 0) -> dict:
    key = jax.random.key(seed)
    ks = jax.random.split(key, 32)
    f32 = jnp.float32

    def nrm(k, shape, fan_in):
        return jax.random.normal(k, shape, f32) * (fan_in ** -0.5)

    def gain(k, shape):
        return 1.0 + 0.02 * jax.random.normal(k, shape, f32)

    dt0 = jnp.exp(jax.random.uniform(ks[10], (DEPTH, 2, SSM_HEADS), f32)
                  * (math.log(DT_MAX) - math.log(DT_MIN)) + math.log(DT_MIN))
    return {
        'x_prompt': jax.random.normal(ks[0], (BATCH, SEQ, D_MODEL), f32),
        'x_sample': jax.random.normal(ks[1], (DEC_BATCH, DEC_SEQ, D_MODEL), f32),
        'p_prompt': jax.random.normal(ks[2], (DEPTH, BATCH, SEQ, PLE_DIM), f32),
        'p_sample': jax.random.normal(ks[3], (DEPTH, DEC_BATCH, DEC_SEQ, PLE_DIM), f32),
        'norm_mix_g': gain(ks[4], (DEPTH, D_MODEL)),
        'w_in': nrm(ks[5], (DEPTH, D_MODEL, W_IN), D_MODEL),
        'diff_lambda': 0.1 * jax.random.normal(ks[6], (DEPTH, 4, HEAD_DIM), f32),
        'diff_subln_g': gain(ks[7], (DEPTH, 2 * HEAD_DIM)),
        'conv_w': nrm(ks[8], (DEPTH, SSM_CONV, C_XBC), SSM_CONV),
        'conv_b': 0.02 * jax.random.normal(ks[9], (DEPTH, C_XBC), f32),
        'dt_bias': dt0 + jnp.log(-jnp.expm1(-dt0)),
        'a_log': jnp.log(jax.random.uniform(ks[11], (DEPTH, 2, SSM_HEADS), f32, minval=1.0, maxval=16.0)),
        'ssm_d': gain(ks[12], (DEPTH, SSM_HEADS)),
        'ssm_norm_g': gain(ks[13], (DEPTH, SSM_INNER)),
        'w_br_a': nrm(ks[14], (DEPTH, A_OUT, D_MODEL), A_OUT),
        'w_br_b': nrm(ks[15], (DEPTH, B_OUT, D_MODEL), B_OUT),
        'w_br_c': nrm(ks[16], (DEPTH, SSM_INNER, D_MODEL), SSM_INNER),
        'w_out': nrm(ks[17], (DEPTH, D_MODEL, D_MODEL), D_MODEL),
        'norm_ffn_g': gain(ks[18], (DEPTH, D_MODEL)),
        'w_ffn_in': nrm(ks[19], (DEPTH, D_MODEL, 2 * FFN_HIDDEN), D_MODEL),
        'w_ffn_out': nrm(ks[20], (DEPTH, FFN_HIDDEN, D_MODEL), FFN_HIDDEN),
        'norm_ple_g': gain(ks[21], (DEPTH, D_MODEL)),
        'w_ple_gate': nrm(ks[22], (DEPTH, D_MODEL, D_MODEL), D_MODEL),
        'w_ple_proj': nrm(ks[23], (DEPTH, PLE_DIM, D_MODEL), PLE_DIM),
        'final_norm_g': gain(ks[24], (D_MODEL,)),
    }


def reference(x_prompt, x_sample, p_prompt, p_sample, norm_mix_g, w_in, diff_lambda, diff_subln_g,
              conv_w, conv_b, dt_bias, a_log, ssm_d, ssm_norm_g, w_br_a, w_br_b, w_br_c, w_out,
              norm_ffn_g, w_ffn_in, w_ffn_out, norm_ple_g, w_ple_gate, w_ple_proj, final_norm_g):
    y_prompt = trunk(x_prompt, p_prompt, norm_mix_g, w_in, diff_lambda, diff_subln_g, conv_w, conv_b,
                     dt_bias, a_log, ssm_d, ssm_norm_g, w_br_a, w_br_b, w_br_c, w_out, norm_ffn_g,
                     w_ffn_in, w_ffn_out, norm_ple_g, w_ple_gate, w_ple_proj, final_norm_g)
    y_sample = trunk(x_sample, p_sample, norm_mix_g, w_in, diff_lambda, diff_subln_g, conv_w, conv_b,
                     dt_bias, a_log, ssm_d, ssm_norm_g, w_br_a, w_br_b, w_br_c, w_out, norm_ffn_g,
                     w_ffn_in, w_ffn_out, norm_ple_g, w_ple_gate, w_ple_proj, final_norm_g)
    return (y_prompt, y_sample)
```

```python
import functools
import math

import jax
import jax.numpy as jnp
from jax import lax
from jax.experimental import pallas as pl
from jax.experimental.pallas import tpu as pltpu

F32 = jnp.float32
BF16 = jnp.bfloat16

D_MODEL = 1024
DEPTH = 2
HEAD_DIM = 64
ROT_DIM = HEAD_DIM // 4
ROPE_THETA = 500000.0
A_HEADS = 4
A_W = A_HEADS * 2 * HEAD_DIM
DIL_PAIRS = ((128, 1), (512, 4), (2048, 16))
N_DIL = len(DIL_PAIRS)
B_HEADS = 8
B_W = B_HEADS * HEAD_DIM
B_QKV = N_DIL * B_W
QKV_W = A_W + B_QKV
SSM_HEADS = 16
SSM_GROUPS = 2
SSM_HPG = SSM_HEADS // SSM_GROUPS
SSM_STATE = 128
SSM_INNER = 1024
SSM_GW = SSM_INNER // SSM_GROUPS
SSM_CONV = 5
SSM_CHUNK = 128
C_XBC = SSM_INNER + 2 * SSM_GROUPS * SSM_STATE
C_DT = 2 * SSM_HEADS
DT_W = SSM_GROUPS * 128
N_BRANCH = 3
GATE_W = N_BRANCH * D_MODEL
SPLITS = (A_W, A_W, A_W, B_QKV, B_QKV, B_QKV, SSM_INNER, C_XBC, C_DT, GATE_W)
FFN_HIDDEN = 2816
PLE_DIM = 256
EPS = 1e-6
NEG = -1e30
LANES = 128
VMEM_LIMIT = 56 * 1024 * 1024


def _cparams(n_axes):
    return pltpu.CompilerParams(
        dimension_semantics=("arbitrary",) * n_axes, vmem_limit_bytes=VMEM_LIMIT)


def _whole_vmem():
    return pl.BlockSpec(memory_space=pltpu.VMEM)


def _sigmoid(x):
    return 1.0 / (1.0 + jnp.exp(-x))


def _rms(x, g):
    return x * lax.rsqrt(jnp.mean(x * x, axis=-1, keepdims=True) + EPS) * g


def _dot(a, b):
    return jnp.dot(a, b, preferred_element_type=F32)


def _dot_nt(a, b):
    return lax.dot_general(a, b, (((1,), (1,)), ((), ())), preferred_element_type=F32)


_INPROJ_CHUNK = 512
_INPROJ_PLAN = (
    [(c, _INPROJ_CHUNK, 0, c, "rope_q") for c in range(0, QKV_W, _INPROJ_CHUNK)]
    + [(QKV_W + c, _INPROJ_CHUNK, 1, c, "rope_k") for c in range(0, QKV_W, _INPROJ_CHUNK)]
    + [(2 * QKV_W + c, _INPROJ_CHUNK, 2, c, "plain") for c in range(0, QKV_W, _INPROJ_CHUNK)]
    + [(3 * QKV_W + c, _INPROJ_CHUNK, 3, c, "plain") for c in range(0, SSM_INNER, _INPROJ_CHUNK)]
    + [(3 * QKV_W + SSM_INNER + c, _INPROJ_CHUNK, 4, c, "plain") for c in range(0, C_XBC, _INPROJ_CHUNK)]
    + [(3 * QKV_W + SSM_INNER + C_XBC, DT_W, 5, 0, "plain")]
    + [(3 * QKV_W + SSM_INNER + C_XBC + DT_W + c, _INPROJ_CHUNK, 6, c, "sigmoid")
       for c in range(0, GATE_W, _INPROJ_CHUNK)]
)
W_IN_COLS = 3 * QKV_W + SSM_INNER + C_XBC + DT_W + GATE_W


def _inproj_kernel(x_ref, g_ref, w_ref, cq_ref, sq_ref, ck_ref, sk_ref, *out_refs):
    h = _rms(x_ref[...], g_ref[...]).astype(BF16)
    lane = lax.broadcasted_iota(jnp.int32, (x_ref.shape[0], LANES), 1)
    first_half = (lane & (HEAD_DIM - 1)) < (ROT_DIM // 2)
    for ws, width, oi, oc, kind in _INPROJ_PLAN:
        acc = _dot(h, w_ref[:, ws:ws + width])
        o_ref = out_refs[oi]
        if kind in ("rope_q", "rope_k"):
            cos = (cq_ref if kind == "rope_q" else ck_ref)[...]
            sin = (sq_ref if kind == "rope_q" else sk_ref)[...]
            for blk in range(width // LANES):
                a = acc[:, blk * LANES:(blk + 1) * LANES]
                partner = jnp.where(first_half, pltpu.roll(a, LANES - ROT_DIM // 2, 1),
                                    pltpu.roll(a, ROT_DIM // 2, 1))
                o_ref[:, oc + blk * LANES:oc + (blk + 1) * LANES] = (
                    a * cos + partner * sin).astype(o_ref.dtype)
        elif kind == "sigmoid":
            o_ref[:, oc:oc + width] = _sigmoid(acc).astype(o_ref.dtype)
        else:
            o_ref[:, oc:oc + width] = acc.astype(o_ref.dtype)


def _inproj(x, g, w, tabs, seq, tm=256):
    t = x.shape[0]
    nseq = seq // tm
    row = lambda i: (i, 0)
    tab = pl.BlockSpec((tm, LANES), lambda i: (i % nseq, 0))
    widths = (QKV_W, QKV_W, QKV_W, SSM_INNER, C_XBC, DT_W, GATE_W)
    dtypes = (BF16, BF16, BF16, BF16, BF16, F32, BF16)
    return pl.pallas_call(
        _inproj_kernel,
        grid=(t // tm,),
        in_specs=[pl.BlockSpec((tm, D_MODEL), row), pl.BlockSpec((1, D_MODEL), lambda i: (0, 0)),
                  _whole_vmem(), tab, tab, tab, tab],
        out_specs=[pl.BlockSpec((tm, wd), row) for wd in widths],
        out_shape=[jax.ShapeDtypeStruct((t, wd), dt) for wd, dt in zip(widths, dtypes)],
        compiler_params=_cparams(1),
        name="inproj",
    )(x, g, w, *tabs)


def _attn_a_kernel(lam_ref, q_ref, k_ref, v_ref, g_ref, o_ref, vext_ref):
    @pl.when(pl.program_id(2) == 0)
    def _():
        vext_ref[:, :LANES] = v_ref[...]
        vext_ref[:, LANES:] = jnp.ones((v_ref.shape[0], LANES), BF16)

    lp = lam_ref[...]
    lam = (jnp.exp(jnp.sum(lp[0:1] * lp[1:2], axis=-1, keepdims=True))
           - jnp.exp(jnp.sum(lp[2:3] * lp[3:4], axis=-1, keepdims=True)) + lp[4:5, 0:1])
    q = q_ref[...]
    k = k_ref[...]
    lane = lax.broadcasted_iota(jnp.int32, q.shape, 1)
    zero = jnp.zeros_like(q)

    def softmax_v(qm):
        s = _dot_nt(qm, k)
        e = jnp.exp(s - jnp.max(s, axis=-1, keepdims=True)).astype(BF16)
        nd = _dot(e, vext_ref[...])
        return nd[:, :LANES] / nd[:, LANES:]

    o = (softmax_v(jnp.where(lane < HEAD_DIM, q, zero))
         - lam * softmax_v(jnp.where(lane >= HEAD_DIM, q, zero)))
    o_ref[...] = (_rms(o, g_ref[...]) * (1.0 - lp[4:5, 0:1])).astype(o_ref.dtype)


def _attn_a(lamrow, q_all, k_all, v_all, g, bsz, seq, tq=256):
    nq = seq // tq
    return pl.pallas_call(
        _attn_a_kernel,
        grid=(bsz, A_HEADS, nq),
        in_specs=[pl.BlockSpec((8, LANES), lambda b, h, i: (0, 0)),
                  pl.BlockSpec((tq, LANES), lambda b, h, i: (b * nq + i, h)),
                  pl.BlockSpec((seq, LANES), lambda b, h, i: (b, h)),
                  pl.BlockSpec((seq, LANES), lambda b, h, i: (b, h)),
                  pl.BlockSpec((1, LANES), lambda b, h, i: (0, 0))],
        out_specs=pl.BlockSpec((tq, LANES), lambda b, h, i: (b * nq + i, h)),
        out_shape=jax.ShapeDtypeStruct((bsz * seq, A_W), BF16),
        scratch_shapes=[pltpu.VMEM((seq, 2 * LANES), BF16)],
        compiler_params=_cparams(3),
        name="attn_a",
    )(lamrow, q_all, k_all, v_all, g)


def _attn_b_kernel(q_ref, k_ref, v_ref, o_ref, lse_ref, *, sub_len, tl, win, radius):
    q0 = pl.program_id(2) * tl
    ws = pl.multiple_of(jnp.clip(q0 - radius, 0, sub_len - win), radius)
    kw = k_ref[pl.ds(ws, win), :]
    vw = v_ref[pl.ds(ws, win), :]
    q = q_ref[...]
    qpos = q0 + lax.broadcasted_iota(jnp.int32, (tl, win), 0)
    kpos = ws + lax.broadcasted_iota(jnp.int32, (tl, win), 1)
    valid = jnp.abs(kpos - qpos) <= radius
    lane = lax.broadcasted_iota(jnp.int32, (tl, LANES), 1)
    lse_acc = jnp.zeros((tl, LANES), F32)
    for j in range(B_W // LANES):
        sl = slice(j * LANES, (j + 1) * LANES)
        qp, kp, vp = q[:, sl], kw[:, sl], vw[:, sl]
        halves = []
        for half in range(2):
            sel = (lane < HEAD_DIM) if half == 0 else (lane >= HEAD_DIM)
            s = jnp.where(valid, _dot_nt(jnp.where(sel, qp, jnp.zeros_like(qp)), kp), NEG)
            m = jnp.max(s, axis=-1, keepdims=True)
            e = jnp.exp(s - m)
            den = jnp.sum(e, axis=-1, keepdims=True)
            halves.append(_dot(e.astype(BF16), vp) / den)
            lse_acc = jnp.where(lane == 2 * j + half, m + jnp.log(den), lse_acc)
        o_ref[:, sl] = jnp.where(lane < HEAD_DIM, halves[0], halves[1]).astype(o_ref.dtype)
    lse_ref[...] = lse_acc


def _attn_b(q_all, k_all, v_all, gi, bsz, seq):
    win_len, dil = DIL_PAIRS[gi]
    radius = win_len // (2 * dil)
    sub_len = seq // dil
    tl = min(128, sub_len)
    win = min(sub_len, tl + 2 * radius)
    nq = sub_len // tl
    cb = QKV_W // B_W
    col = lambda r: r * cb + 1 + gi
    view = lambda a: a.reshape(bsz * sub_len, dil * QKV_W)
    kern = functools.partial(_attn_b_kernel, sub_len=sub_len, tl=tl, win=win, radius=radius)
    o, lse = pl.pallas_call(
        kern,
        grid=(bsz, dil, nq),
        in_specs=[pl.BlockSpec((tl, B_W), lambda b, r, i: (b * nq + i, col(r))),
                  pl.BlockSpec((sub_len, B_W), lambda b, r, i: (b, col(r))),
                  pl.BlockSpec((sub_len, B_W), lambda b, r, i: (b, col(r)))],
        out_specs=[pl.BlockSpec((tl, B_W), lambda b, r, i: (b * nq + i, r)),
                   pl.BlockSpec((None, None, tl, LANES), lambda b, r, i: (b, r, i, 0))],
        out_shape=[jax.ShapeDtypeStruct((bsz * sub_len, dil * B_W), BF16),
                   jax.ShapeDtypeStruct((bsz, dil, sub_len, LANES), F32)],
        compiler_params=_cparams(3),
        name=f"attn_b{gi}",
    )(view(q_all), view(k_all), view(v_all))
    lse = lse.transpose(0, 2, 1, 3).reshape(bsz * seq, LANES)
    return o.reshape(bsz * seq, B_W), lse


_CONV_ROWS = 256
_CONV_HALO = 16


def _conv_kernel(x_ref, w_ref, b_ref, o_ref, *, seq):
    n = seq // _CONV_ROWS
    w = w_ref[...]
    bias = b_ref[...]
    pad = (SSM_CONV - 1) // 2

    def body(c, carry):
        r0 = pl.multiple_of(c * _CONV_ROWS, _CONV_ROWS)
        main = x_ref[pl.ds(r0, _CONV_ROWS), :].astype(F32)
        lo = pl.multiple_of(jnp.maximum(r0 - _CONV_HALO, 0), _CONV_HALO)
        hi = pl.multiple_of(jnp.minimum(r0 + _CONV_ROWS, seq - _CONV_HALO), _CONV_HALO)
        prev = jnp.where(c > 0, x_ref[pl.ds(lo, _CONV_HALO), :].astype(F32), 0.0)
        nxt = jnp.where(c < n - 1, x_ref[pl.ds(hi, _CONV_HALO), :].astype(F32), 0.0)
        ext = jnp.concatenate([prev, main, nxt], axis=0)
        y = bias
        for kk in range(SSM_CONV):
            st = _CONV_HALO + kk - pad
            y = y + w[kk:kk + 1, :] * ext[st:st + _CONV_ROWS, :]
        o_ref[pl.ds(r0, _CONV_ROWS), :] = (y * _sigmoid(y)).astype(o_ref.dtype)
        return carry

    lax.fori_loop(0, n, body, 0)


def _conv(xbc, w, b, bsz, seq, tc=512):
    return pl.pallas_call(
        functools.partial(_conv_kernel, seq=seq),
        grid=(bsz, C_XBC // tc),
        in_specs=[pl.BlockSpec((seq, tc), lambda b_, j: (b_, j)),
                  pl.BlockSpec((SSM_CONV, tc), lambda b_, j: (0, j)),
                  pl.BlockSpec((1, tc), lambda b_, j: (0, j))],
        out_specs=pl.BlockSpec((seq, tc), lambda b_, j: (b_, j)),
        out_shape=jax.ShapeDtypeStruct(xbc.shape, BF16),
        compiler_params=_cparams(2),
        name="conv",
    )(xbc, w, b)


def _split3(x):
    hi = x.astype(BF16)
    r1 = x - hi.astype(F32)
    mid = r1.astype(BF16)
    lo = (r1 - mid.astype(F32)).astype(BF16)
    return hi, mid, lo


def _ssd_kernel(x_ref, b_ref, c_ref, z_ref, dt_ref, prm_ref, o_ref, y_ref, st_ref, *, seq):
    q = SSM_CHUNK
    nchunk = seq // q
    npair = SSM_GW // LANES
    ri = lax.broadcasted_iota(jnp.int32, (q, q), 0)
    ci = lax.broadcasted_iota(jnp.int32, (q, q), 1)
    lower = ri >= ci
    upper = ci >= ri
    tri = jnp.where(lower, 1.0, 0.0).astype(BF16)
    lo_half = lax.broadcasted_iota(jnp.int32, (q, LANES), 1) < HEAD_DIM
    lo_half_row = lo_half[0:1]
    dt_bias = prm_ref[0:1, :LANES]
    a_row = -jnp.exp(prm_ref[1:2, :LANES])
    d_row = prm_ref[2:3, :]
    gn_row = prm_ref[3:4, :]

    def colpair(arr, a):
        return jnp.where(lo_half, arr[:, a:a + 1], arr[:, a + 1:a + 2])

    def rowpair(row, a):
        return jnp.where(lo_half_row, row[:, a:a + 1], row[:, a + 1:a + 2])

    def chunk_terms(c):
        r0 = pl.multiple_of(c * q, q)
        rows = pl.ds(r0, q)
        raw = dt_ref[rows, :] + dt_bias
        dtp = jnp.maximum(raw, 0.0) + jnp.log(1.0 + jnp.exp(-jnp.abs(raw)))
        adt = dtp * a_row
        hi, mid, lo = _split3(adt)
        cs = _dot(tri, hi) + _dot(tri, mid) + _dot(tri, lo)
        bc = b_ref[rows, :]
        cc = c_ref[rows, :]
        cb = _dot_nt(cc, bc)
        bt = bc.astype(F32).T.astype(BF16)
        return rows, dtp, adt, cs, cc, cb, bt

    def sweep(c, backward):
        rows, dtp, adt, cs, cc, cb, bt = chunk_terms(c)
        tot = cs[q - 1:q, :]
        base = SSM_HPG if backward else 0
        pos = cs - adt if backward else cs
        pos_t = pos.T
        for j in range(npair):
            a = base + 2 * j
            sl = slice(j * LANES, (j + 1) * LANES)
            pos_p = colpair(pos, a)
            tot_p = rowpair(tot, a)
            xin = x_ref[rows, sl].astype(F32)
            xs = xin * colpair(dtp, a)
            xsb = xs.astype(BF16)
            yd = []
            for hh in range(2):
                col = pos[:, a + hh:a + hh + 1]
                row = pos_t[a + hh:a + hh + 1, :]
                if backward:
                    lm = jnp.where(upper, jnp.exp(row - col), 0.0)
                else:
                    lm = jnp.where(lower, jnp.exp(col - row), 0.0)
                yd.append(_dot((cb * lm).astype(BF16), xsb))
            st = st_ref[j]
            if backward:
                off_scale = jnp.exp(tot_p - pos_p)
                in_scale = jnp.exp(pos_p)
            else:
                off_scale = jnp.exp(pos_p)
                in_scale = jnp.exp(tot_p - pos_p)
            y = jnp.where(lo_half, yd[0], yd[1]) + _dot(cc, st.astype(BF16)) * off_scale
            st_ref[j] = st * jnp.exp(tot_p) + _dot(bt, (xs * in_scale).astype(BF16))
            if backward:
                y_ref[rows, sl] = y_ref[rows, sl] + y + d_row[:, sl] * xin
            else:
                y_ref[rows, sl] = y
        if backward:
            zc = z_ref[rows, :].astype(F32)
            y = y_ref[rows, :] * (zc * _sigmoid(zc))
            o_ref[rows, :] = _rms(y, gn_row).astype(o_ref.dtype)

    st_ref[...] = jnp.zeros_like(st_ref)
    lax.fori_loop(0, nchunk, lambda c, u: (sweep(c, False), u)[1], 0)
    st_ref[...] = jnp.zeros_like(st_ref)
    lax.fori_loop(0, nchunk, lambda c, u: (sweep(nchunk - 1 - c, True), u)[1], 0)


def _ssd(xbc, z, dt, prm, bsz, seq):
    xcb = SSM_INNER // LANES
    return pl.pallas_call(
        functools.partial(_ssd_kernel, seq=seq),
        grid=(bsz, SSM_GROUPS),
        in_specs=[pl.BlockSpec((seq, SSM_GW), lambda b, g: (b, g)),
                  pl.BlockSpec((seq, SSM_STATE), lambda b, g: (b, xcb + g)),
                  pl.BlockSpec((seq, SSM_STATE), lambda b, g: (b, xcb + SSM_GROUPS + g)),
                  pl.BlockSpec((seq, SSM_GW), lambda b, g: (b, g)),
                  pl.BlockSpec((seq, LANES), lambda b, g: (b, g)),
                  pl.BlockSpec((None, 8, SSM_GW), lambda b, g: (g, 0, 0))],
        out_specs=pl.BlockSpec((seq, SSM_GW), lambda b, g: (b, g)),
        out_shape=jax.ShapeDtypeStruct((bsz * seq, SSM_INNER), BF16),
        scratch_shapes=[pltpu.VMEM((seq, SSM_GW), F32),
                        pltpu.VMEM((SSM_GW // LANES, SSM_STATE, LANES), F32)],
        compiler_params=_cparams(2),
        name="ssd",
    )(xbc, xbc, xbc, z, dt, prm)


def _merge_kernel(oa_ref, o0_ref, o1_ref, o2_ref, l0_ref, l1_ref, l2_ref, oc_ref, gate_ref, x_ref,
                  e_ref, wa_ref, wb_ref, wc_ref, wo_ref, out_ref):
    l0, l1, l2 = l0_ref[...], l1_ref[...], l2_ref[...]
    mx = jnp.maximum(jnp.maximum(l0, l1), l2)
    e0, e1, e2 = jnp.exp(l0 - mx), jnp.exp(l1 - mx), jnp.exp(l2 - mx)
    inv = 1.0 / (e0 + e1 + e2)
    ob = jnp.zeros(o0_ref.shape, F32)
    for e, o_ref in ((e0, o0_ref), (e1, o1_ref), (e2, o2_ref)):
        w_hi, w_mid, _ = _split3(e * inv)
        wfull = _dot(w_hi, e_ref[...]) + _dot(w_mid, e_ref[...])
        ob = ob + wfull * o_ref[...].astype(F32)
    m = (gate_ref[:, :D_MODEL].astype(F32) * _dot(oa_ref[...], wa_ref[...])
         + gate_ref[:, D_MODEL:2 * D_MODEL].astype(F32) * _dot(ob.astype(BF16), wb_ref[...])
         + gate_ref[:, 2 * D_MODEL:].astype(F32) * _dot(oc_ref[...], wc_ref[...]))
    out_ref[...] = x_ref[...] + _dot(m.astype(BF16), wo_ref[...])


def _merge(oa, obs, lses, oc, gates, x, expand, wa, wb, wc, wo, tm=512):
    t = x.shape[0]
    row = lambda i: (i, 0)
    const = lambda i: (0, 0)
    blk = lambda a: pl.BlockSpec((tm, a.shape[1]), row)
    full = lambda a: pl.BlockSpec(a.shape, const)
    args = (oa, *obs, *lses, oc, gates, x, expand, wa, wb, wc, wo)
    specs = [blk(a) for a in args[:10]] + [full(a) for a in args[10:]]
    return pl.pallas_call(
        _merge_kernel, grid=(t // tm,), in_specs=specs,
        out_specs=pl.BlockSpec((tm, D_MODEL), row),
        out_shape=jax.ShapeDtypeStruct((t, D_MODEL), F32),
        compiler_params=_cparams(1), name="merge",
    )(*args)


_FFN_CHUNKS = 2


def _ffn_kernel(x_ref, g_ref, wi_ref, wo_ref, out_ref):
    x = x_ref[...]
    h = _rms(x, g_ref[...]).astype(BF16)
    cw = FFN_HIDDEN // _FFN_CHUNKS
    acc = x
    for c in range(_FFN_CHUNKS):
        gt = _dot(h, wi_ref[:, c * cw:(c + 1) * cw])
        up = _dot(h, wi_ref[:, FFN_HIDDEN + c * cw:FFN_HIDDEN + (c + 1) * cw])
        acc = acc + _dot((gt * _sigmoid(gt) * up).astype(BF16), wo_ref[c * cw:(c + 1) * cw, :])
    out_ref[...] = acc


def _ffn(x, g, wi, wo, tm=256):
    t = x.shape[0]
    row = lambda i: (i, 0)
    return pl.pallas_call(
        _ffn_kernel, grid=(t // tm,),
        in_specs=[pl.BlockSpec((tm, D_MODEL), row), pl.BlockSpec((1, D_MODEL), lambda i: (0, 0)),
                  _whole_vmem(), _whole_vmem()],
        out_specs=pl.BlockSpec((tm, D_MODEL), row),
        out_shape=jax.ShapeDtypeStruct((t, D_MODEL), F32),
        compiler_params=_cparams(1), name="ffn",
    )(x, g, wi, wo)


def _ple_kernel(x_ref, p_ref, g_ref, wg_ref, wp_ref, fg_ref, out_ref, *, final):
    x = x_ref[...]
    pg = _sigmoid(_dot(_rms(x, g_ref[...]).astype(BF16), wg_ref[...]))
    y = x + pg * _dot(p_ref[...].astype(BF16), wp_ref[...])
    out_ref[...] = _rms(y, fg_ref[...]) if final else y


def _ple(x, p, g, wg, wp, fg, final, tm=512):
    t = x.shape[0]
    row = lambda i: (i, 0)
    const = lambda i: (0, 0)
    return pl.pallas_call(
        functools.partial(_ple_kernel, final=final), grid=(t // tm,),
        in_specs=[pl.BlockSpec((tm, D_MODEL), row), pl.BlockSpec((tm, PLE_DIM), row),
                  pl.BlockSpec((1, D_MODEL), const), pl.BlockSpec(wg.shape, const),
                  pl.BlockSpec(wp.shape, const), pl.BlockSpec((1, D_MODEL), const)],
        out_specs=pl.BlockSpec((tm, D_MODEL), row),
        out_shape=jax.ShapeDtypeStruct((t, D_MODEL), F32),
        compiler_params=_cparams(1), name="ple",
    )(x, p, g, wg, wp, fg)


def _rope_tables(seq):
    half = ROT_DIM // 2
    inv = ROPE_THETA ** (-jnp.arange(0, ROT_DIM, 2, dtype=F32) / ROT_DIM)
    ang = jnp.arange(seq, dtype=F32)[:, None] * inv[None, :]
    cos, sin = jnp.cos(ang), jnp.sin(ang)
    ones = jnp.ones((seq, HEAD_DIM - ROT_DIM), F32)
    cos_h = jnp.concatenate([cos, cos, ones], axis=1)
    sin_h = jnp.concatenate([-sin, sin, 0.0 * ones], axis=1)
    cos_t = jnp.tile(cos_h, (1, LANES // HEAD_DIM))
    sin_t = jnp.tile(sin_h, (1, LANES // HEAD_DIM))
    scale = HEAD_DIM ** -0.5
    return cos_t * scale, sin_t * scale, cos_t, sin_t


def _layer_params(li, norm_mix_g, w_in, diff_lambda, diff_subln_g, conv_w, conv_b, dt_bias, a_log,
                  ssm_d, ssm_norm_g, w_br_a, w_br_b, w_br_c, w_out, norm_ffn_g, w_ffn_in, w_ffn_out,
                  norm_ple_g, w_ple_gate, w_ple_proj):
    pts = [0]
    for s in SPLITS:
        pts.append(pts[-1] + s)
    seg = [w_in[li][:, pts[i]:pts[i + 1]] for i in range(len(SPLITS))]
    a_q, a_k, a_v, b_q, b_k, b_v, c_z, c_xbc, c_dt, g_raw = seg
    dt_cols = []
    for grp in range(SSM_GROUPS):
        hs = slice(grp * SSM_HPG, (grp + 1) * SSM_HPG)
        dt_cols += [c_dt[:, :SSM_HEADS][:, hs], c_dt[:, SSM_HEADS:][:, hs],
                    jnp.zeros((D_MODEL, LANES - 2 * SSM_HPG), F32)]
    w = jnp.concatenate([a_q, b_q, a_k, b_k, a_v, b_v, c_z, c_xbc, *dt_cols, g_raw],
                        axis=1).astype(BF16)
    lam_init = 0.8 - 0.6 * math.exp(-0.3 * li)
    lamrow = jnp.concatenate(
        [jnp.pad(diff_lambda[li], ((0, 0), (0, LANES - HEAD_DIM))),
         jnp.full((4, LANES), lam_init, F32)], axis=0)
    rows = []
    for grp in range(SSM_GROUPS):
        hs = slice(grp * SSM_HPG, (grp + 1) * SSM_HPG)
        padw = SSM_GW - 2 * SSM_HPG
        pad_pair = lambda v: jnp.pad(jnp.concatenate([v[0, hs], v[1, hs]]), (0, padw))
        rows.append(jnp.stack(
            [pad_pair(dt_bias[li]), pad_pair(a_log[li]), jnp.repeat(ssm_d[li][hs], HEAD_DIM),
             ssm_norm_g[li][grp * SSM_GW:(grp + 1) * SSM_GW]] + [jnp.zeros((SSM_GW,), F32)] * 4))
    return dict(
        norm_mix_g=norm_mix_g[li][None], w_in=w, lamrow=lamrow, subln_g=diff_subln_g[li][None],
        conv_w=conv_w[li], conv_b=conv_b[li][None], ssd_prm=jnp.stack(rows),
        w_br_a=w_br_a[li].astype(BF16), w_br_b=w_br_b[li].astype(BF16),
        w_br_c=w_br_c[li].astype(BF16), w_out=w_out[li].astype(BF16),
        norm_ffn_g=norm_ffn_g[li][None], w_ffn_in=w_ffn_in[li].astype(BF16),
        w_ffn_out=w_ffn_out[li].astype(BF16), norm_ple_g=norm_ple_g[li][None],
        w_ple_gate=w_ple_gate[li].astype(BF16), w_ple_proj=w_ple_proj[li].astype(BF16))


def _trunk(x, p, layers, final_g, expand):
    bsz, seq, _ = x.shape
    t = bsz * seq
    x = x.reshape(t, D_MODEL)
    tabs = _rope_tables(seq)
    for li, lp in enumerate(layers):
        q_all, k_all, v_all, z, xbc, dt, gates = _inproj(x, lp["norm_mix_g"], lp["w_in"], tabs, seq)
        oa = _attn_a(lp["lamrow"], q_all, k_all, v_all, lp["subln_g"], bsz, seq)
        obs, lses = zip(*[_attn_b(q_all, k_all, v_all, gi, bsz, seq) for gi in range(N_DIL)])
        oc = _ssd(_conv(xbc, lp["conv_w"], lp["conv_b"], bsz, seq), z, dt, lp["ssd_prm"], bsz, seq)
        x = _merge(oa, obs, lses, oc, gates, x, expand, lp["w_br_a"], lp["w_br_b"], lp["w_br_c"],
                   lp["w_out"])
        x = _ffn(x, lp["norm_ffn_g"], lp["w_ffn_in"], lp["w_ffn_out"])
        x = _ple(x, p[li].reshape(t, PLE_DIM), lp["norm_ple_g"], lp["w_ple_gate"],
                 lp["w_ple_proj"], final_g, final=(li == len(layers) - 1))
    return x.reshape(bsz, seq, D_MODEL)


def kernel(x_prompt, x_sample, p_prompt, p_sample, norm_mix_g, w_in, diff_lambda, diff_subln_g, conv_w, conv_b, dt_bias, a_log, ssm_d, ssm_norm_g, w_br_a, w_br_b, w_br_c, w_out, norm_ffn_g, w_ffn_in, w_ffn_out, norm_ple_g, w_ple_gate, w_ple_proj, final_norm_g):
    layers = [_layer_params(li, norm_mix_g, w_in, diff_lambda, diff_subln_g, conv_w, conv_b, dt_bias,
                            a_log, ssm_d, ssm_norm_g, w_br_a, w_br_b, w_br_c, w_out, norm_ffn_g,
                            w_ffn_in, w_ffn_out, norm_ple_g, w_ple_gate, w_ple_proj)
              for li in range(DEPTH)]
    head_of_lane = jnp.arange(B_W) // HEAD_DIM
    expand = (jnp.arange(LANES)[:, None] == head_of_lane[None, :]).astype(BF16)
    fg = final_norm_g[None]
    return (_trunk(x_prompt, p_prompt, layers, fg, expand),
            _trunk(x_sample, p_sample, layers, fg, expand))
```

```python
import functools
import math

import jax
import jax.numpy as jnp
from jax import lax
from jax.experimental import pallas as pl
from jax.experimental.pallas import tpu as pltpu

F32 = jnp.float32
BF16 = jnp.bfloat16

D_MODEL = 1024
DEPTH = 2
HEAD_DIM = 64
ROT_DIM = HEAD_DIM // 4
ROPE_THETA = 500000.0
A_HEADS = 4
A_W = A_HEADS * 2 * HEAD_DIM
DIL_PAIRS = ((128, 1), (512, 4), (2048, 16))
N_DIL = len(DIL_PAIRS)
B_HEADS = 8
B_W = B_HEADS * HEAD_DIM
B_QKV = N_DIL * B_W
QKV_W = A_W + B_QKV
SSM_HEADS = 16
SSM_GROUPS = 2
SSM_HPG = SSM_HEADS // SSM_GROUPS
SSM_STATE = 128
SSM_INNER = 1024
SSM_GW = SSM_INNER // SSM_GROUPS
SSM_CONV = 5
SSM_CHUNK = 128
C_XBC = SSM_INNER + 2 * SSM_GROUPS * SSM_STATE
C_DT = 2 * SSM_HEADS
DT_W = SSM_GROUPS * 128
N_BRANCH = 3
GATE_W = N_BRANCH * D_MODEL
SPLITS = (A_W, A_W, A_W, B_QKV, B_QKV, B_QKV, SSM_INNER, C_XBC, C_DT, GATE_W)
FFN_HIDDEN = 2816
PLE_DIM = 256
EPS = 1e-6
NEG = -1e30
LN2 = math.log(2.0)
LOG2E = 1.0 / LN2
LANES = 128
VMEM_LIMIT = 56 * 1024 * 1024


def _cparams(n_axes):
    return pltpu.CompilerParams(
        dimension_semantics=("arbitrary",) * n_axes, vmem_limit_bytes=VMEM_LIMIT)


def _whole_vmem():
    return pl.BlockSpec(memory_space=pltpu.VMEM)


def _sigmoid(x):
    return 1.0 / (1.0 + jnp.exp(-x))


def _rms(x, g):
    return x * lax.rsqrt(jnp.mean(x * x, axis=-1, keepdims=True) + EPS) * g


def _dot(a, b):
    return jnp.dot(a, b, preferred_element_type=F32)


def _dot_nt(a, b):
    return lax.dot_general(a, b, (((1,), (1,)), ((), ())), preferred_element_type=F32)


TOKEN_TILE = 256
DILS = tuple(d for _, d in DIL_PAIRS)
_INPROJ_CHUNK = 512


def _inproj_plan():
    plan, ws = [], 0
    for oi, kind in ((0, "rope_q"), (1, "rope_k"), (2, "plain")):
        for c in range(0, NAT_W, _INPROJ_CHUNK):
            plan.append((ws, _INPROJ_CHUNK, oi, c, kind, 0))
            ws += _INPROJ_CHUNK
    for gi in range(1, N_DIL):
        for j, kind in enumerate(("rope_q", "rope_k", "plain")):
            plan.append((ws, B_W, 3 * gi + j, 0, kind, gi))
            ws += B_W
    base = 3 * N_DIL
    for oi, total, kind in ((base, SSM_INNER, "plain"), (base + 1, C_XBC, "plain"),
                            (base + 2, DT_W, "plain"), (base + 3, GATE_W, "sigmoid")):
        for c in range(0, total, _INPROJ_CHUNK):
            wd = min(_INPROJ_CHUNK, total - c)
            plan.append((ws, wd, oi, c, kind, 0))
            ws += wd
    return plan, ws


NAT_W = A_W + B_W
_INPROJ_PLAN, W_IN_COLS = _inproj_plan()
_N_TABS = 4 * N_DIL


def _inproj_kernel(x_ref, g_ref, w_ref, *refs):
    perm_refs = refs[:N_DIL - 1]
    tab_refs = refs[N_DIL - 1:N_DIL - 1 + _N_TABS]
    out_refs = refs[N_DIL - 1 + _N_TABS:]
    tm = x_ref.shape[0]
    h = _rms(x_ref[...], g_ref[...]).astype(BF16)
    hs = [h] + [_dot(p_ref[...], h).astype(BF16) for p_ref in perm_refs]
    lane = lax.broadcasted_iota(jnp.int32, (tm, LANES), 1)
    first_half = (lane & (HEAD_DIM - 1)) < (ROT_DIM // 2)
    for ws, width, oi, oc, kind, order in _INPROJ_PLAN:
        acc = _dot(hs[order], w_ref[:, ws:ws + width])
        o_ref = out_refs[oi]
        dil = DILS[order]

        def put(c0, c1, val):
            val = val.astype(o_ref.dtype)
            if order == 0:
                o_ref[:, oc + c0:oc + c1] = val
            else:
                o_ref[:, :, oc + c0:oc + c1] = val.reshape(dil, tm // dil, c1 - c0)

        if kind in ("rope_q", "rope_k"):
            t0 = 4 * order + (0 if kind == "rope_q" else 2)
            cos, sin = tab_refs[t0][...], tab_refs[t0 + 1][...]
            for blk in range(width // LANES):
                a = acc[:, blk * LANES:(blk + 1) * LANES]
                partner = jnp.where(first_half, pltpu.roll(a, LANES - ROT_DIM // 2, 1),
                                    pltpu.roll(a, ROT_DIM // 2, 1))
                put(blk * LANES, (blk + 1) * LANES, a * cos + partner * sin)
        elif kind == "sigmoid":
            put(0, width, _sigmoid(acc))
        else:
            put(0, width, acc)


def _inproj(x, g, w, perms, tabs, bsz, seq):
    tm = TOKEN_TILE
    t = x.shape[0]
    nseq = seq // tm
    row = lambda i: (i, 0)
    const = lambda i: (0, 0)
    tab = pl.BlockSpec((tm, LANES), lambda i: (i % nseq, 0))
    specs, shapes = [], []
    for _ in range(3):
        specs.append(pl.BlockSpec((tm, NAT_W), row))
        shapes.append(jax.ShapeDtypeStruct((t, NAT_W), BF16))
    for gi in range(1, N_DIL):
        d = DILS[gi]
        for _ in range(3):
            specs.append(pl.BlockSpec((None, d, tm // d, B_W),
                                      lambda i: (i // nseq, 0, i % nseq, 0)))
            shapes.append(jax.ShapeDtypeStruct((bsz, d, seq // d, B_W), BF16))
    for wd, dt in ((SSM_INNER, BF16), (C_XBC, BF16), (DT_W, F32), (GATE_W, BF16)):
        specs.append(pl.BlockSpec((tm, wd), row))
        shapes.append(jax.ShapeDtypeStruct((t, wd), dt))
    return pl.pallas_call(
        _inproj_kernel,
        grid=(t // tm,),
        in_specs=[pl.BlockSpec((tm, D_MODEL), row), pl.BlockSpec((1, D_MODEL), const), _whole_vmem()]
        + [pl.BlockSpec((tm, tm), const)] * len(perms) + [tab] * len(tabs),
        out_specs=specs, out_shape=shapes,
        compiler_params=_cparams(1),
        name="inproj",
    )(x, g, w, *perms, *tabs)


_A_KCHUNK = 512
_A_VROWS = LANES + 16


def _attn_a_kernel(lam_ref, q_ref, k_ref, v_ref, g_ref, o_ref, vt_ref, *, seq):
    kc = _A_KCHUNK
    nk = seq // kc

    @pl.when(pl.program_id(2) == 0)
    def _():
        for c in range(nk):
            vt_ref[:LANES, c * kc:(c + 1) * kc] = (
                v_ref[c * kc:(c + 1) * kc, :].astype(F32).T.astype(BF16))
        vt_ref[LANES:, :] = jnp.ones((_A_VROWS - LANES, seq), BF16)

    lp = lam_ref[...]
    lam = (jnp.exp(jnp.sum(lp[0:1] * lp[1:2], axis=-1, keepdims=True))
           - jnp.exp(jnp.sum(lp[2:3] * lp[3:4], axis=-1, keepdims=True)) + lp[4:5, 0:1])
    q = q_ref[...]
    lane = lax.broadcasted_iota(jnp.int32, q.shape, 1)
    zero = jnp.zeros_like(q)
    tq = q.shape[0]
    qcat = jnp.concatenate(
        [jnp.where(lane < HEAD_DIM, q, zero), jnp.where(lane >= HEAD_DIM, q, zero)], axis=0)
    acc = jnp.zeros((_A_VROWS, 2 * tq), F32)
    m = jnp.full((1, 2 * tq), NEG, F32)
    scores = lambda j: _dot_nt(k_ref[j * kc:(j + 1) * kc, :], qcat)
    s_next = scores(0)
    for j in range(nk):
        s = s_next
        if j + 1 < nk:
            s_next = scores(j + 1)
        m_new = jnp.maximum(m, jnp.max(s, axis=0, keepdims=True))
        p = jnp.exp2(s - m_new).astype(BF16)
        acc = acc * jnp.exp2(m - m_new) + _dot(vt_ref[:, j * kc:(j + 1) * kc], p)
        m = m_new
    a0, a1 = acc[:, :tq], acc[:, tq:]
    ot = a0[:LANES] / a0[LANES:LANES + 1] - lam * (a1[:LANES] / a1[LANES:LANES + 1])
    o_ref[...] = (_rms(ot.T, g_ref[...]) * (1.0 - lp[4:5, 0:1])).astype(o_ref.dtype)


def _attn_a(lamrow, q_nat, k_nat, v_nat, g, bsz, seq, tq=256):
    nq = seq // tq
    return pl.pallas_call(
        functools.partial(_attn_a_kernel, seq=seq),
        grid=(bsz, A_HEADS, nq),
        in_specs=[pl.BlockSpec((8, LANES), lambda b, h, i: (0, 0)),
                  pl.BlockSpec((tq, LANES), lambda b, h, i: (b * nq + i, h)),
                  pl.BlockSpec((seq, LANES), lambda b, h, i: (b, h)),
                  pl.BlockSpec((seq, LANES), lambda b, h, i: (b, h)),
                  pl.BlockSpec((1, LANES), lambda b, h, i: (0, 0))],
        out_specs=pl.BlockSpec((tq, LANES), lambda b, h, i: (b * nq + i, h)),
        out_shape=jax.ShapeDtypeStruct((bsz * seq, A_W), BF16),
        scratch_shapes=[pltpu.VMEM((_A_VROWS, seq), BF16)],
        compiler_params=_cparams(3),
        name="attn_a",
    )(lamrow, q_nat, k_nat, v_nat, g)


def _attn_b_kernel(q_ref, k_ref, v_ref, o_ref, lse_ref, *, sub_len, tl, win, radius):
    q0 = pl.program_id(2) * tl
    ws = pl.multiple_of(jnp.clip(q0 - radius, 0, sub_len - win), radius)
    kw = k_ref[pl.ds(ws, win), :]
    vw = v_ref[pl.ds(ws, win), :]
    q = q_ref[...]
    qpos = q0 + lax.broadcasted_iota(jnp.int32, (tl, win), 0)
    kpos = ws + lax.broadcasted_iota(jnp.int32, (tl, win), 1)
    valid = jnp.abs(kpos - qpos) <= radius
    lane = lax.broadcasted_iota(jnp.int32, (tl, LANES), 1)
    lse_acc = jnp.zeros((tl, LANES), F32)
    for j in range(B_W // LANES):
        sl = slice(j * LANES, (j + 1) * LANES)
        qp, kp, vp = q[:, sl], kw[:, sl], vw[:, sl]
        halves = []
        for half in range(2):
            sel = (lane < HEAD_DIM) if half == 0 else (lane >= HEAD_DIM)
            s = jnp.where(valid, _dot_nt(jnp.where(sel, qp, jnp.zeros_like(qp)), kp), NEG)
            m = jnp.max(s, axis=-1, keepdims=True)
            e = jnp.exp2(s - m)
            den = jnp.sum(e, axis=-1, keepdims=True)
            halves.append(_dot(e.astype(BF16), vp) / den)
            lse_acc = jnp.where(lane == 2 * j + half, (m + jnp.log2(den)) * LN2, lse_acc)
        o_ref[:, sl] = jnp.where(lane < HEAD_DIM, halves[0], halves[1]).astype(o_ref.dtype)
    hi, mid, lo = _split3(lse_acc)
    lse_ref[...] = (hi.astype(F32) + pltpu.roll(mid.astype(F32), B_HEADS, 1)
                    + pltpu.roll(lo.astype(F32), 2 * B_HEADS, 1)).astype(lse_ref.dtype)


def _attn_b(q, k, v, gi, bsz, seq):
    win_len, dil = DIL_PAIRS[gi]
    radius = win_len // (2 * dil)
    sub_len = seq // dil
    tl = min(128, sub_len)
    win = min(sub_len, tl + 2 * radius)
    nq = sub_len // tl
    kern = functools.partial(_attn_b_kernel, sub_len=sub_len, tl=tl, win=win, radius=radius)
    if gi == 0:
        q_spec = pl.BlockSpec((tl, B_W), lambda b, r, i: (b * nq + i, 1))
        kv_spec = pl.BlockSpec((sub_len, B_W), lambda b, r, i: (b, 1))
        out_specs = [pl.BlockSpec((tl, B_W), lambda b, r, i: (b * nq + i, 0)),
                     pl.BlockSpec((tl, LANES), lambda b, r, i: (b * nq + i, 0))]
        out_shape = [jax.ShapeDtypeStruct((bsz * seq, B_W), BF16),
                     jax.ShapeDtypeStruct((bsz * seq, LANES), BF16)]
    else:
        q_spec = pl.BlockSpec((None, None, tl, B_W), lambda b, r, i: (b, r, i, 0))
        kv_spec = pl.BlockSpec((None, None, sub_len, B_W), lambda b, r, i: (b, r, 0, 0))
        out_specs = [q_spec, pl.BlockSpec((None, None, tl, LANES), lambda b, r, i: (b, r, i, 0))]
        out_shape = [jax.ShapeDtypeStruct((bsz, dil, sub_len, B_W), BF16),
                     jax.ShapeDtypeStruct((bsz, dil, sub_len, LANES), BF16)]
    return pl.pallas_call(
        kern, grid=(bsz, dil, nq), in_specs=[q_spec, kv_spec, kv_spec],
        out_specs=out_specs, out_shape=out_shape,
        compiler_params=_cparams(3), name=f"attn_b{gi}",
    )(q, k, v)


_CONV_ROWS = 256
_CONV_HALO = 16


def _conv_kernel(x_ref, w_ref, b_ref, o_ref, *, seq):
    n = seq // _CONV_ROWS
    w = w_ref[...]
    bias = b_ref[...]
    pad = (SSM_CONV - 1) // 2

    def body(c, carry):
        r0 = pl.multiple_of(c * _CONV_ROWS, _CONV_ROWS)
        main = x_ref[pl.ds(r0, _CONV_ROWS), :].astype(F32)
        lo = pl.multiple_of(jnp.maximum(r0 - _CONV_HALO, 0), _CONV_HALO)
        hi = pl.multiple_of(jnp.minimum(r0 + _CONV_ROWS, seq - _CONV_HALO), _CONV_HALO)
        prev = jnp.where(c > 0, x_ref[pl.ds(lo, _CONV_HALO), :].astype(F32), 0.0)
        nxt = jnp.where(c < n - 1, x_ref[pl.ds(hi, _CONV_HALO), :].astype(F32), 0.0)
        ext = jnp.concatenate([prev, main, nxt], axis=0)
        y = bias
        for kk in range(SSM_CONV):
            st = _CONV_HALO + kk - pad
            y = y + w[kk:kk + 1, :] * ext[st:st + _CONV_ROWS, :]
        o_ref[pl.ds(r0, _CONV_ROWS), :] = (y * _sigmoid(y)).astype(o_ref.dtype)
        return carry

    lax.fori_loop(0, n, body, 0)


def _conv(xbc, w, b, bsz, seq, tc=512):
    return pl.pallas_call(
        functools.partial(_conv_kernel, seq=seq),
        grid=(bsz, C_XBC // tc),
        in_specs=[pl.BlockSpec((seq, tc), lambda b_, j: (b_, j)),
                  pl.BlockSpec((SSM_CONV, tc), lambda b_, j: (0, j)),
                  pl.BlockSpec((1, tc), lambda b_, j: (0, j))],
        out_specs=pl.BlockSpec((seq, tc), lambda b_, j: (b_, j)),
        out_shape=jax.ShapeDtypeStruct(xbc.shape, BF16),
        compiler_params=_cparams(2),
        name="conv",
    )(xbc, w, b)


def _split3(x):
    hi = x.astype(BF16)
    r1 = x - hi.astype(F32)
    mid = r1.astype(BF16)
    lo = (r1 - mid.astype(F32)).astype(BF16)
    return hi, mid, lo


_SSD_UNROLL = 4


def _ssd_kernel(x_ref, b_ref, c_ref, z_ref, dt_ref, prm_ref, o_ref, y_ref, st_ref, *, seq):
    q = SSM_CHUNK
    nchunk = seq // q
    npair = SSM_GW // LANES
    ri = lax.broadcasted_iota(jnp.int32, (q, q), 0)
    ci = lax.broadcasted_iota(jnp.int32, (q, q), 1)
    lower = ri >= ci
    upper = ci >= ri
    tri = jnp.where(lower, 1.0, 0.0).astype(BF16)
    lo_half = lax.broadcasted_iota(jnp.int32, (q, LANES), 1) < HEAD_DIM
    lo_half_row = lo_half[0:1]
    dt_bias = prm_ref[0:1, :LANES]
    a_row = -jnp.exp(prm_ref[1:2, :LANES])
    d_row = prm_ref[2:3, :]
    gn_row = prm_ref[3:4, :]

    def colpair(arr, a):
        return jnp.where(lo_half, arr[:, a:a + 1], arr[:, a + 1:a + 2])

    def rowpair(row, a):
        return jnp.where(lo_half_row, row[:, a:a + 1], row[:, a + 1:a + 2])

    head_of_lane = lax.broadcasted_iota(jnp.int32, (q, SSM_GW), 1) // HEAD_DIM

    def chunk_terms(c):
        r0 = pl.multiple_of(c * q, q)
        rows = pl.ds(r0, q)
        raw = dt_ref[rows, :] + dt_bias
        dtp = jnp.maximum(raw, 0.0) + jnp.log(1.0 + jnp.exp(-jnp.abs(raw)))
        adt = dtp * a_row
        cs3 = _dot(tri, jnp.concatenate(_split3(adt), axis=1))
        cs = cs3[:, :LANES] + cs3[:, LANES:2 * LANES] + cs3[:, 2 * LANES:]
        bc = b_ref[rows, :]
        cc = c_ref[rows, :]
        cb = _dot_nt(cc, bc)
        bt = bc.astype(F32).T.astype(BF16)
        return rows, dtp, adt, cs, cc, cb, bt

    def intra(terms, backward):
        rows, dtp, adt, cs, cc, cb, bt = terms
        tot = cs[q - 1:q, :]
        base = SSM_HPG if backward else 0
        pos = cs - adt if backward else cs
        pos_t = pos.T
        heads = lambda f, arr: jnp.concatenate(
            [f(arr, base + 2 * j) for j in range(npair)], axis=1)
        pos_all, dt_all, tot_all = heads(colpair, pos), heads(colpair, dtp), heads(rowpair, tot)
        xin = x_ref[rows, :].astype(F32)
        xs = xin * dt_all
        xsb = xs.astype(BF16)
        lms, xblocks = [], []
        for h in range(SSM_HPG):
            col = pos[:, base + h:base + h + 1]
            row = pos_t[base + h:base + h + 1, :]
            if backward:
                lm = jnp.where(upper, jnp.exp(row - col), 0.0)
            else:
                lm = jnp.where(lower, jnp.exp(col - row), 0.0)
            lms.append((cb * lm).astype(BF16))
            xblocks.append(jnp.where(head_of_lane == h, xsb, jnp.zeros_like(xsb)))
        y = _dot(jnp.concatenate(lms, axis=1), jnp.concatenate(xblocks, axis=0))
        if backward:
            off_scale = jnp.exp(tot_all - pos_all)
            in_scale = jnp.exp(pos_all)
            y = y + d_row * xin
        else:
            off_scale = jnp.exp(pos_all)
            in_scale = jnp.exp(tot_all - pos_all)
        return rows, y, cc, off_scale, bt, (xs * in_scale).astype(BF16), jnp.exp(tot_all)

    def sweep(first, backward):
        chunks = [nchunk - 1 - (first + u) if backward else first + u for u in range(_SSD_UNROLL)]
        parts = [intra(terms, backward) for terms in [chunk_terms(c) for c in chunks]]
        st = st_ref[...]
        for rows, y, cc, off_scale, bt, xd, decay in parts:
            y = y + _dot(cc, st.astype(BF16)) * off_scale
            st = st * decay + _dot(bt, xd)
            if backward:
                zc = z_ref[rows, :].astype(F32)
                y = (y_ref[rows, :] + y) * (zc * _sigmoid(zc))
                o_ref[rows, :] = _rms(y, gn_row).astype(o_ref.dtype)
            else:
                y_ref[rows, :] = y
        st_ref[...] = st

    for backward in (False, True):
        st_ref[...] = jnp.zeros_like(st_ref)
        lax.fori_loop(0, nchunk // _SSD_UNROLL,
                      lambda i, u: (sweep(i * _SSD_UNROLL, backward), u)[1], 0)


def _ssd(xbc, z, dt, prm, bsz, seq):
    xcb = SSM_INNER // LANES
    return pl.pallas_call(
        functools.partial(_ssd_kernel, seq=seq),
        grid=(bsz, SSM_GROUPS),
        in_specs=[pl.BlockSpec((seq, SSM_GW), lambda b, g: (b, g)),
                  pl.BlockSpec((seq, SSM_STATE), lambda b, g: (b, xcb + g)),
                  pl.BlockSpec((seq, SSM_STATE), lambda b, g: (b, xcb + SSM_GROUPS + g)),
                  pl.BlockSpec((seq, SSM_GW), lambda b, g: (b, g)),
                  pl.BlockSpec((seq, LANES), lambda b, g: (b, g)),
                  pl.BlockSpec((None, 8, SSM_GW), lambda b, g: (g, 0, 0))],
        out_specs=pl.BlockSpec((seq, SSM_GW), lambda b, g: (b, g)),
        out_shape=jax.ShapeDtypeStruct((bsz * seq, SSM_INNER), BF16),
        scratch_shapes=[pltpu.VMEM((seq, SSM_GW), F32),
                        pltpu.VMEM((SSM_STATE, SSM_GW), F32)],
        compiler_params=_cparams(2),
        name="ssd",
    )(xbc, xbc, xbc, z, dt, prm)


def _merge_kernel(*refs):
    oa_ref = refs[0]
    o_refs = refs[1:1 + N_DIL]
    l_refs = refs[1 + N_DIL:1 + 2 * N_DIL]
    oc_ref, gate_ref, x_ref = refs[1 + 2 * N_DIL:4 + 2 * N_DIL]
    unperm_refs = refs[4 + 2 * N_DIL:3 + 3 * N_DIL]
    e_ref, wa_ref, wb_ref, wc_ref, wo_ref, out_ref = refs[3 + 3 * N_DIL:]
    tm = x_ref.shape[0]
    lane = lax.broadcasted_iota(jnp.int32, (tm, LANES), 1)
    outs, lses = [], []
    for gi in range(N_DIL):
        o = o_refs[gi][...]
        packed = l_refs[gi][...]
        if gi == 0:
            o, packed = o.astype(F32), packed.astype(F32)
        else:
            un = unperm_refs[gi - 1][...]
            o = _dot(un, o.reshape(tm, B_W))
            packed = _dot(un, packed.reshape(tm, LANES))
        lse = (packed + pltpu.roll(packed, LANES - B_HEADS, 1)
               + pltpu.roll(packed, LANES - 2 * B_HEADS, 1))
        outs.append(o)
        lses.append(jnp.where(lane < B_HEADS, lse, 0.0))
    mx = functools.reduce(jnp.maximum, lses)
    es = [jnp.exp(l - mx) for l in lses]
    inv = 1.0 / functools.reduce(lambda a, b: a + b, es)
    ob = jnp.zeros((tm, B_W), F32)
    for e, o in zip(es, outs):
        ob = ob + _dot((e * inv).astype(BF16), e_ref[...]) * o
    m = (gate_ref[:, :D_MODEL].astype(F32) * _dot(oa_ref[...], wa_ref[...])
         + gate_ref[:, D_MODEL:2 * D_MODEL].astype(F32) * _dot(ob.astype(BF16), wb_ref[...])
         + gate_ref[:, 2 * D_MODEL:].astype(F32) * _dot(oc_ref[...], wc_ref[...]))
    out_ref[...] = x_ref[...] + _dot(m.astype(BF16), wo_ref[...])


def _merge(oa, obs, lses, oc, gates, x, unperms, expand, wa, wb, wc, wo, bsz, seq):
    tm = TOKEN_TILE
    t = x.shape[0]
    nseq = seq // tm
    row = lambda i: (i, 0)
    const = lambda i: (0, 0)
    blk = lambda a: pl.BlockSpec((tm, a.shape[-1]), row)
    full = lambda a: pl.BlockSpec(a.shape, const)

    def grouped(a, gi):
        if gi == 0:
            return blk(a)
        return pl.BlockSpec((None, DILS[gi], tm // DILS[gi], a.shape[-1]),
                            lambda i: (i // nseq, 0, i % nseq, 0))

    specs = ([blk(oa)] + [grouped(a, gi) for gi, a in enumerate(obs)]
             + [grouped(a, gi) for gi, a in enumerate(lses)] + [blk(oc), blk(gates), blk(x)]
             + [full(a) for a in (*unperms, expand, wa, wb, wc, wo)])
    return pl.pallas_call(
        _merge_kernel, grid=(t // tm,), in_specs=specs,
        out_specs=pl.BlockSpec((tm, D_MODEL), row),
        out_shape=jax.ShapeDtypeStruct((t, D_MODEL), F32),
        compiler_params=_cparams(1), name="merge",
    )(oa, *obs, *lses, oc, gates, x, *unperms, expand, wa, wb, wc, wo)


_FFN_CHUNKS = 2


def _ffn_kernel(x_ref, g_ref, wi_ref, wo_ref, out_ref):
    x = x_ref[...]
    h = _rms(x, g_ref[...]).astype(BF16)
    cw = FFN_HIDDEN // _FFN_CHUNKS
    acc = x
    for c in range(_FFN_CHUNKS):
        gt = _dot(h, wi_ref[:, c * cw:(c + 1) * cw])
        up = _dot(h, wi_ref[:, FFN_HIDDEN + c * cw:FFN_HIDDEN + (c + 1) * cw])
        acc = acc + _dot((gt * _sigmoid(gt) * up).astype(BF16), wo_ref[c * cw:(c + 1) * cw, :])
    out_ref[...] = acc


def _ffn(x, g, wi, wo, tm=256):
    t = x.shape[0]
    row = lambda i: (i, 0)
    return pl.pallas_call(
        _ffn_kernel, grid=(t // tm,),
        in_specs=[pl.BlockSpec((tm, D_MODEL), row), pl.BlockSpec((1, D_MODEL), lambda i: (0, 0)),
                  _whole_vmem(), _whole_vmem()],
        out_specs=pl.BlockSpec((tm, D_MODEL), row),
        out_shape=jax.ShapeDtypeStruct((t, D_MODEL), F32),
        compiler_params=_cparams(1), name="ffn",
    )(x, g, wi, wo)


def _ple_kernel(x_ref, p_ref, g_ref, wg_ref, wp_ref, fg_ref, out_ref, *, final):
    x = x_ref[...]
    pg = _sigmoid(_dot(_rms(x, g_ref[...]).astype(BF16), wg_ref[...]))
    y = x + pg * _dot(p_ref[...].astype(BF16), wp_ref[...])
    out_ref[...] = _rms(y, fg_ref[...]) if final else y


def _ple(x, p, g, wg, wp, fg, final, tm=512):
    t = x.shape[0]
    row = lambda i: (i, 0)
    const = lambda i: (0, 0)
    return pl.pallas_call(
        functools.partial(_ple_kernel, final=final), grid=(t // tm,),
        in_specs=[pl.BlockSpec((tm, D_MODEL), row), pl.BlockSpec((tm, PLE_DIM), row),
                  pl.BlockSpec((1, D_MODEL), const), pl.BlockSpec(wg.shape, const),
                  pl.BlockSpec(wp.shape, const), pl.BlockSpec((1, D_MODEL), const)],
        out_specs=pl.BlockSpec((tm, D_MODEL), row),
        out_shape=jax.ShapeDtypeStruct((t, D_MODEL), F32),
        compiler_params=_cparams(1), name="ple",
    )(x, p, g, wg, wp, fg)


def _rope_tables(seq):
    inv = ROPE_THETA ** (-jnp.arange(0, ROT_DIM, 2, dtype=F32) / ROT_DIM)
    ang = jnp.arange(seq, dtype=F32)[:, None] * inv[None, :]
    cos, sin = jnp.cos(ang), jnp.sin(ang)
    ones = jnp.ones((seq, HEAD_DIM - ROT_DIM), F32)
    cos_h = jnp.concatenate([cos, cos, ones], axis=1)
    sin_h = jnp.concatenate([-sin, sin, 0.0 * ones], axis=1)
    cos_t = jnp.tile(cos_h, (1, LANES // HEAD_DIM))
    sin_t = jnp.tile(sin_h, (1, LANES // HEAD_DIM))
    scale = HEAD_DIM ** -0.5 * LOG2E
    tabs = []
    for d in DILS:
        for tab in (cos_t * scale, sin_t * scale, cos_t, sin_t):
            tabs.append(tab.reshape(seq // TOKEN_TILE, TOKEN_TILE // d, d, LANES)
                        .transpose(0, 2, 1, 3).reshape(seq, LANES))
    return tabs


def _perm_matrices():
    n = jnp.arange(TOKEN_TILE)
    perms = []
    for d in DILS[1:]:
        src = d * (n % (TOKEN_TILE // d)) + n // (TOKEN_TILE // d)
        perms.append((src[:, None] == n[None, :]).astype(BF16))
    return perms, [p.T for p in perms]


def _layer_params(li, norm_mix_g, w_in, diff_lambda, diff_subln_g, conv_w, conv_b, dt_bias, a_log,
                  ssm_d, ssm_norm_g, w_br_a, w_br_b, w_br_c, w_out, norm_ffn_g, w_ffn_in, w_ffn_out,
                  norm_ple_g, w_ple_gate, w_ple_proj):
    pts = [0]
    for s in SPLITS:
        pts.append(pts[-1] + s)
    seg = [w_in[li][:, pts[i]:pts[i + 1]] for i in range(len(SPLITS))]
    a_q, a_k, a_v, b_q, b_k, b_v, c_z, c_xbc, c_dt, g_raw = seg
    dt_cols = []
    for grp in range(SSM_GROUPS):
        hs = slice(grp * SSM_HPG, (grp + 1) * SSM_HPG)
        dt_cols += [c_dt[:, :SSM_HEADS][:, hs], c_dt[:, SSM_HEADS:][:, hs],
                    jnp.zeros((D_MODEL, LANES - 2 * SSM_HPG), F32)]
    grp_cols = lambda m, gi: m[:, gi * B_W:(gi + 1) * B_W]
    cols = [a_q, grp_cols(b_q, 0), a_k, grp_cols(b_k, 0), a_v, grp_cols(b_v, 0)]
    for gi in range(1, N_DIL):
        cols += [grp_cols(b_q, gi), grp_cols(b_k, gi), grp_cols(b_v, gi)]
    w = jnp.concatenate(cols + [c_z, c_xbc, *dt_cols, g_raw], axis=1).astype(BF16)
    lam_init = 0.8 - 0.6 * math.exp(-0.3 * li)
    lamrow = jnp.concatenate(
        [jnp.pad(diff_lambda[li], ((0, 0), (0, LANES - HEAD_DIM))),
         jnp.full((4, LANES), lam_init, F32)], axis=0)
    rows = []
    for grp in range(SSM_GROUPS):
        hs = slice(grp * SSM_HPG, (grp + 1) * SSM_HPG)
        padw = SSM_GW - 2 * SSM_HPG
        pad_pair = lambda v: jnp.pad(jnp.concatenate([v[0, hs], v[1, hs]]), (0, padw))
        rows.append(jnp.stack(
            [pad_pair(dt_bias[li]), pad_pair(a_log[li]), jnp.repeat(ssm_d[li][hs], HEAD_DIM),
             ssm_norm_g[li][grp * SSM_GW:(grp + 1) * SSM_GW]] + [jnp.zeros((SSM_GW,), F32)] * 4))
    return dict(
        norm_mix_g=norm_mix_g[li][None], w_in=w, lamrow=lamrow, subln_g=diff_subln_g[li][None],
        conv_w=conv_w[li], conv_b=conv_b[li][None], ssd_prm=jnp.stack(rows),
        w_br_a=w_br_a[li].astype(BF16), w_br_b=w_br_b[li].astype(BF16),
        w_br_c=w_br_c[li].astype(BF16), w_out=w_out[li].astype(BF16),
        norm_ffn_g=norm_ffn_g[li][None], w_ffn_in=w_ffn_in[li].astype(BF16),
        w_ffn_out=w_ffn_out[li].astype(BF16), norm_ple_g=norm_ple_g[li][None],
        w_ple_gate=w_ple_gate[li].astype(BF16), w_ple_proj=w_ple_proj[li].astype(BF16))


def _trunk(x, p, layers, final_g, expand):
    bsz, seq, _ = x.shape
    t = bsz * seq
    x = x.reshape(t, D_MODEL)
    tabs = _rope_tables(seq)
    perms, unperms = _perm_matrices()
    for li, lp in enumerate(layers):
        outs = _inproj(x, lp["norm_mix_g"], lp["w_in"], perms, tabs, bsz, seq)
        qkv = [outs[3 * gi:3 * gi + 3] for gi in range(N_DIL)]
        z, xbc, dt, gates = outs[3 * N_DIL:]
        oa = _attn_a(lp["lamrow"], *qkv[0], lp["subln_g"], bsz, seq)
        obs, lses = zip(*[_attn_b(*qkv[gi], gi, bsz, seq) for gi in range(N_DIL)])
        oc = _ssd(_conv(xbc, lp["conv_w"], lp["conv_b"], bsz, seq), z, dt, lp["ssd_prm"], bsz, seq)
        x = _merge(oa, obs, lses, oc, gates, x, unperms, expand, lp["w_br_a"], lp["w_br_b"],
                   lp["w_br_c"], lp["w_out"], bsz, seq)
        x = _ffn(x, lp["norm_ffn_g"], lp["w_ffn_in"], lp["w_ffn_out"])
        x = _ple(x, p[li].reshape(t, PLE_DIM), lp["norm_ple_g"], lp["w_ple_gate"],
                 lp["w_ple_proj"], final_g, final=(li == len(layers) - 1))
    return x.reshape(bsz, seq, D_MODEL)


def kernel(x_prompt, x_sample, p_prompt, p_sample, norm_mix_g, w_in, diff_lambda, diff_subln_g, conv_w, conv_b, dt_bias, a_log, ssm_d, ssm_norm_g, w_br_a, w_br_b, w_br_c, w_out, norm_ffn_g, w_ffn_in, w_ffn_out, norm_ple_g, w_ple_gate, w_ple_proj, final_norm_g):
    layers = [_layer_params(li, norm_mix_g, w_in, diff_lambda, diff_subln_g, conv_w, conv_b, dt_bias,
                            a_log, ssm_d, ssm_norm_g, w_br_a, w_br_b, w_br_c, w_out, norm_ffn_g,
                            w_ffn_in, w_ffn_out, norm_ple_g, w_ple_gate, w_ple_proj)
              for li in range(DEPTH)]
    head_of_lane = jnp.arange(B_W) // HEAD_DIM
    expand = (jnp.arange(LANES)[:, None] == head_of_lane[None, :]).astype(BF16)
    fg = final_norm_g[None]
    return (_trunk(x_prompt, p_prompt, layers, fg, expand),
            _trunk(x_sample, p_sample, layers, fg, expand))
```
